```python
import math
import jax, jax.numpy as jnp
from jax import lax
import numpy as np

D_MODEL = 1024
BATCH = 8
SEQ = 4096
DEPTH = 1

N_MEM = 256
RET_HEADS = 4
RET_DV = D_MODEL // 2 // RET_HEADS
RET_DK = RET_DV // 2
RET_QK_W = RET_HEADS * RET_DK
RET_V_W = RET_HEADS * RET_DV
DIFF_HEADS = 4
DIFF_DH = D_MODEL // 2 // DIFF_HEADS // 2
DIFF_DV = 2 * DIFF_DH
DIFF_QK_W = DIFF_HEADS * 2 * DIFF_DH
DIFF_V_W = DIFF_HEADS * DIFF_DV
MIX_W = RET_V_W + DIFF_V_W
IN_COLS = 2 * RET_QK_W + 2 * RET_V_W + 2 * DIFF_QK_W + DIFF_V_W
XATTN_HEADS = 4
XATTN_DH = D_MODEL // XATTN_HEADS
D_FF = ((8 * D_MODEL // 3 + 127) // 128) * 128
RET_CHUNK = 128
Q_BLOCK = 128
ROPE_BASE = 10000.0
EPS = 1e-6

kernel_name = "hymba_retnet_diffattn_macaron"


def rmsnorm(x, g):
    xf = x.astype(jnp.float32)
    y = xf * lax.rsqrt(jnp.mean(xf * xf, axis=-1, keepdims=True) + EPS)
    return (y * g.astype(jnp.float32)).astype(x.dtype)


def swiglu(h, w_gate, w_up, w_down):
    return (jax.nn.silu(h @ w_gate) * (h @ w_up)) @ w_down


def rotary(t, pos):
    d = t.shape[-1]
    inv = 1.0 / (ROPE_BASE ** (jnp.arange(0, d, 2, dtype=jnp.float32) / d))
    ang = pos.astype(jnp.float32)[:, None] * inv[None, :]
    cos = jnp.cos(ang)[None, :, None, :].astype(t.dtype)
    sin = jnp.sin(ang)[None, :, None, :].astype(t.dtype)
    t1, t2 = t[..., : d // 2], t[..., d // 2:]
    return jnp.concatenate([t1 * cos - t2 * sin, t1 * sin + t2 * cos], axis=-1)


def retention_chunkwise(q, k, v):
    B, S, H, dk = q.shape
    dv = v.shape[-1]
    C = RET_CHUNK
    NC = S // C
    log_g = jnp.log(1.0 - 2.0 ** (-5.0 - jnp.arange(H, dtype=jnp.float32)))
    n = jnp.arange(C, dtype=jnp.float32)
    rel = n[:, None] - n[None, :]
    decay = jnp.where(rel[None] >= 0, jnp.exp(jnp.maximum(rel, 0.0)[None] * log_g[:, None, None]), 0.0)
    xi = jnp.exp((n[None, :] + 1.0) * log_g[:, None])
    zeta = jnp.exp((C - 1.0 - n[None, :]) * log_g[:, None])
    g_chunk = jnp.exp(C * log_g)

    def to_chunks(t):
        return t.astype(jnp.float32).reshape(B, NC, C, H, t.shape[-1]).transpose(1, 0, 3, 2, 4)

    qc, kc, vc = to_chunks(q), to_chunks(k), to_chunks(v)

    def step(R, inp):
        qi, ki, vi = inp
        inner = jnp.einsum('bhnd,bhmd->bhnm', qi, ki) * decay[None]
        o = jnp.einsum('bhnm,bhme->bhne', inner, vi)
        o = o + jnp.einsum('bhnd,bhde->bhne', qi, R) * xi[None, :, :, None]
        R = g_chunk[None, :, None, None] * R + jnp.einsum('bhmd,bhme->bhde', ki * zeta[None, :, :, None], vi)
        return R, o

    R0 = jnp.zeros((B, H, dk, dv), jnp.float32)
    _, out = lax.scan(step, R0, (qc, kc, vc))
    return out.transpose(1, 0, 3, 2, 4).reshape(B, S, H, dv)


def diff_attention(q, k, v, lam):
    B, S, H, _, dh = q.shape
    NB = S // Q_BLOCK
    scale = 1.0 / math.sqrt(dh)
    qb = q.reshape(B, NB, Q_BLOCK, H, 2, dh).transpose(1, 0, 2, 3, 4, 5)
    kpos = jnp.arange(S)

    def block(args):
        qi, i = args
        s = jnp.einsum('bqhcd,bkhcd->bhcqk', qi, k).astype(jnp.float32) * scale
        qpos = i * Q_BLOCK + jnp.arange(Q_BLOCK)
        mask = kpos[None, :] <= qpos[:, None]
        s = jnp.where(mask[None, None, None], s, -jnp.inf)
        p = jax.nn.softmax(s, axis=-1)
        a = p[:, :, 0] - lam * p[:, :, 1]
        return jnp.einsum('bhqk,bkhe->bqhe', a.astype(v.dtype), v)

    out = lax.map(block, (qb, jnp.arange(NB)))
    return out.transpose(1, 0, 2, 3, 4).reshape(B, S, H, v.shape[-1])


def memory_cross_attention(h, m, wq, wkv, wo):
    B, S, _ = h.shape
    M = m.shape[1]
    q = (h @ wq).reshape(B, S, XATTN_HEADS, XATTN_DH)
    kv = m @ wkv
    k = kv[..., :D_MODEL].reshape(B, M, XATTN_HEADS, XATTN_DH)
    v = kv[..., D_MODEL:].reshape(B, M, XATTN_HEADS, XATTN_DH)
    s = jnp.einsum('bshd,bmhd->bhsm', q, k).astype(jnp.float32) / math.sqrt(XATTN_DH)
    p = jax.nn.softmax(s, axis=-1).astype(v.dtype)
    o = jnp.einsum('bhsm,bmhd->bshd', p, v).reshape(B, S, D_MODEL)
    return o @ wo


def setup_inputs(seed: int = 0) -> dict:
    key = jax.random.key(seed)
    ks = jax.random.split(key, 32)
    f32 = jnp.float32

    def w(k, shape, fan_in):
        return jax.random.normal(k, shape, f32) * (fan_in ** -0.5)

    def gain(k, shape):
        return 1.0 + 0.02 * jax.random.normal(k, shape, f32)

    L = DEPTH
    return {
        "x": jax.random.normal(ks[0], (BATCH, SEQ, D_MODEL), f32),
        "mem": jax.random.normal(ks[1], (BATCH, N_MEM, D_MODEL), f32),
        "ffn1_norm": gain(ks[2], (L, D_MODEL)),
        "ffn1_w_gate": w(ks[3], (L, D_MODEL, D_FF), D_MODEL),
        "ffn1_w_up": w(ks[4], (L, D_MODEL, D_FF), D_MODEL),
        "ffn1_w_down": w(ks[5], (L, D_FF, D_MODEL), D_FF),
        "mix_norm": gain(ks[6], (L, D_MODEL)),
        "w_in": w(ks[7], (L, D_MODEL, IN_COLS), D_MODEL),
        "ret_out_norm": gain(ks[8], (L, RET_HEADS, RET_DV)),
        "diff_lq1": 0.1 * jax.random.normal(ks[9], (L, DIFF_DH), f32),
        "diff_lk1": 0.1 * jax.random.normal(ks[10], (L, DIFF_DH), f32),
        "diff_lq2": 0.1 * jax.random.normal(ks[11], (L, DIFF_DH), f32),
        "diff_lk2": 0.1 * jax.random.normal(ks[12], (L, DIFF_DH), f32),
        "diff_out_norm": gain(ks[13], (L, DIFF_HEADS, DIFF_DV)),
        "w_out": w(ks[14], (L, MIX_W, D_MODEL), MIX_W),
        "xattn_norm": gain(ks[15], (L, D_MODEL)),
        "mem_norm": gain(ks[16], (L, D_MODEL)),
        "xattn_wq": w(ks[17], (L, D_MODEL, D_MODEL), D_MODEL),
        "xattn_wkv": w(ks[18], (L, D_MODEL, 2 * D_MODEL), D_MODEL),
        "xattn_wo": w(ks[19], (L, D_MODEL, D_MODEL), D_MODEL),
        "ffn2_norm": gain(ks[20], (L, D_MODEL)),
        "ffn2_w_gate": w(ks[21], (L, D_MODEL, D_FF), D_MODEL),
        "ffn2_w_up": w(ks[22], (L, D_MODEL, D_FF), D_MODEL),
        "ffn2_w_down": w(ks[23], (L, D_FF, D_MODEL), D_FF),
        "final_norm": gain(ks[24], (D_MODEL,)),
    }


def reference(x, mem, ffn1_norm, ffn1_w_gate, ffn1_w_up, ffn1_w_down, mix_norm, w_in,
              ret_out_norm, diff_lq1, diff_lk1, diff_lq2, diff_lk2, diff_out_norm, w_out,
              xattn_norm, mem_norm, xattn_wq, xattn_wkv, xattn_wo,
              ffn2_norm, ffn2_w_gate, ffn2_w_up, ffn2_w_down, final_norm):
    B, S, _ = x.shape
    pos = jnp.arange(S)
    split_at = np.cumsum([RET_QK_W, RET_QK_W, RET_V_W, RET_V_W, DIFF_QK_W, DIFF_QK_W]).tolist()
    for l in range(DEPTH):
        x = x + 0.5 * swiglu(rmsnorm(x, ffn1_norm[l]), ffn1_w_gate[l], ffn1_w_up[l], ffn1_w_down[l])

        h = rmsnorm(x, mix_norm[l])
        p = h @ w_in[l]
        rq, rk, rv, rg, dq, dk, dv = jnp.split(p, split_at, axis=-1)

        rq = rotary(rq.reshape(B, S, RET_HEADS, RET_DK), pos)
        rk = rotary(rk.reshape(B, S, RET_HEADS, RET_DK), pos) * (RET_DK ** -0.5)
        rv = rv.reshape(B, S, RET_HEADS, RET_DV)
        y_ret = retention_chunkwise(rq, rk, rv).astype(x.dtype)
        y_ret = rmsnorm(y_ret, ret_out_norm[l]).reshape(B, S, RET_V_W) * jax.nn.silu(rg)

        lambda_init = 0.8 - 0.6 * math.exp(-0.3 * l)
        lam = (jnp.exp(jnp.sum(diff_lq1[l].astype(jnp.float32) * diff_lk1[l].astype(jnp.float32)))
               - jnp.exp(jnp.sum(diff_lq2[l].astype(jnp.float32) * diff_lk2[l].astype(jnp.float32)))
               + lambda_init)
        dq = dq.reshape(B, S, DIFF_HEADS, 2, DIFF_DH)
        dk = dk.reshape(B, S, DIFF_HEADS, 2, DIFF_DH)
        dv = dv.reshape(B, S, DIFF_HEADS, DIFF_DV)
        y_diff = diff_attention(dq, dk, dv, lam)
        y_diff = (rmsnorm(y_diff, diff_out_norm[l]) * (1.0 - lambda_init)).reshape(B, S, DIFF_V_W)

        x = x + jnp.concatenate([y_ret, y_diff], axis=-1) @ w_out[l]

        x = x + memory_cross_attention(rmsnorm(x, xattn_norm[l]), rmsnorm(mem, mem_norm[l]),
                                       xattn_wq[l], xattn_wkv[l], xattn_wo[l])

        x = x + 0.5 * swiglu(rmsnorm(x, ffn2_norm[l]), ffn2_w_gate[l], ffn2_w_up[l], ffn2_w_down[l])
    return rmsnorm(x, final_norm)
```

```python
import functools
import math

import jax
import jax.numpy as jnp
import numpy as np
from jax import lax
from jax.experimental import pallas as pl
from jax.experimental.pallas import tpu as pltpu

F32 = jnp.float32
BF16 = jnp.bfloat16

D_MODEL = 1024
N_MEM = 256
RET_HEADS = 4
RET_DK = 64
RET_DV = 128
DIFF_HEADS = 4
DIFF_DH = 64
DIFF_DV = 128
IN_COLS = 3072
XATTN_HEADS = 4
XATTN_DH = 256
ROPE_BASE = 10000.0
EPS = 1e-6
LAMBDA_INIT = 0.8 - 0.6 * math.exp(-0.3 * 0)

LANES = 128
N_COL_BLOCKS = IN_COLS // LANES
BLK_DQ, BLK_DK, BLK_DV = 12, 16, 20

TOKEN_TILE = 512
RET_CHUNK = 256
ATT_TILE = 512
VMEM_LIMIT = 56 * 1024 * 1024


def _rms(x, g):
    ms = jnp.mean(x * x, axis=-1, keepdims=True)
    return x * lax.rsqrt(ms + EPS) * g


def _dot(a, b):
    return jnp.dot(a, b, preferred_element_type=F32)


def _dot_nt(a, b):
    return lax.dot_general(a, b, (((1,), (1,)), ((), ())), preferred_element_type=F32)


def _resident(shape):
    nd = len(shape)
    return pl.BlockSpec(shape, lambda *_: (0,) * nd, pipeline_mode=pl.Buffered(1))


def _params(*sem):
    return pltpu.CompilerParams(dimension_semantics=sem, vmem_limit_bytes=VMEM_LIMIT)


def _ffn_kernel(x_ref, g_ref, wg_ref, wu_ref, wd_ref, gf_ref, o_ref, *, final):
    x = x_ref[...]
    h = _rms(x, g_ref[...]).astype(BF16)
    a = _dot(h, wg_ref[...])
    u = _dot(h, wu_ref[...])
    act = (a * jax.nn.sigmoid(a) * u).astype(BF16)
    y = x + 0.5 * _dot(act, wd_ref[...])
    if final:
        y = _rms(y, gf_ref[...])
    o_ref[...] = y


def _ffn(x, g, wg, wu, wd, gf, final):
    t, d = x.shape
    tile = pl.BlockSpec((TOKEN_TILE, d), lambda i: (i, 0))
    return pl.pallas_call(
        functools.partial(_ffn_kernel, final=final),
        grid=(t // TOKEN_TILE,),
        in_specs=[tile, _resident(g.shape), _resident(wg.shape), _resident(wu.shape),
                  _resident(wd.shape), _resident(gf.shape)],
        out_specs=tile,
        out_shape=jax.ShapeDtypeStruct((t, d), F32),
        compiler_params=_params("parallel"),
        name="ffn_final" if final else "ffn",
    )(x, g, wg, wu, wd, gf)


def _inproj_kernel(x_ref, g_ref, w_ref, cos_ref, sa_ref, sb_ref, o_ref):
    h = _rms(x_ref[...], g_ref[...]).astype(BF16)
    p = _dot(h, w_ref[...])
    cos, sa, sb = cos_ref[...], sa_ref[...], sb_ref[...]
    qk_w = RET_HEADS * RET_DK

    def rot(t):
        up = pltpu.roll(t, qk_w - RET_DK // 2, 1)
        down = pltpu.roll(t, RET_DK // 2, 1)
        return t * cos + up * sa + down * sb

    rq = rot(p[:, 0:qk_w])
    rk = rot(p[:, qk_w:2 * qk_w]) * (RET_DK ** -0.5)
    for j in range(2):
        o_ref[j] = rq[:, j * LANES:(j + 1) * LANES].astype(BF16)
        o_ref[2 + j] = rk[:, j * LANES:(j + 1) * LANES].astype(BF16)
    for j in range(4, 8):
        o_ref[j] = p[:, j * LANES:(j + 1) * LANES].astype(BF16)
    for j in range(8, 12):
        gcol = p[:, j * LANES:(j + 1) * LANES]
        o_ref[j] = (gcol * jax.nn.sigmoid(gcol)).astype(BF16)
    for j in range(BLK_DQ, BLK_DK):
        o_ref[j] = (p[:, j * LANES:(j + 1) * LANES] * (DIFF_DH ** -0.5)).astype(BF16)
    for j in range(BLK_DK, N_COL_BLOCKS):
        o_ref[j] = p[:, j * LANES:(j + 1) * LANES].astype(BF16)


def _rope_tables(seq):
    half = RET_DK // 2
    inv = 1.0 / (ROPE_BASE ** (jnp.arange(0, RET_DK, 2, dtype=F32) / RET_DK))
    ang = jnp.arange(seq, dtype=F32)[:, None] * inv[None, :]
    cos, sin = jnp.cos(ang), jnp.sin(ang)
    zero = jnp.zeros_like(sin)
    tile = lambda a, b: jnp.tile(jnp.concatenate([a, b], axis=-1), (1, RET_HEADS))
    assert half * 2 == RET_DK
    return tile(cos, cos), tile(-sin, zero), tile(zero, sin)


def _inproj(x, g, w, seq):
    t, d = x.shape
    steps_per_seq = seq // TOKEN_TILE
    tables = _rope_tables(seq)
    tab_spec = pl.BlockSpec((TOKEN_TILE, RET_HEADS * RET_DK), lambda i: (i % steps_per_seq, 0))
    return pl.pallas_call(
        _inproj_kernel,
        grid=(t // TOKEN_TILE,),
        in_specs=[pl.BlockSpec((TOKEN_TILE, d), lambda i: (i, 0)), _resident(g.shape),
                  _resident(w.shape), tab_spec, tab_spec, tab_spec],
        out_specs=pl.BlockSpec((N_COL_BLOCKS, TOKEN_TILE, LANES), lambda i: (0, i, 0)),
        out_shape=jax.ShapeDtypeStruct((N_COL_BLOCKS, t, LANES), BF16),
        compiler_params=_params("parallel"),
        name="inproj",
    )(x, g, w, *tables)


def _retention_kernel(q_ref, k_ref, v_ref, gate_ref, decay_ref, xi_ref, zeta_ref, g_ref,
                      o_ref, state_ref, *, g_chunk):
    @pl.when(pl.program_id(1) == 0)
    def _():
        state_ref[...] = jnp.zeros_like(state_ref)

    lane = lax.broadcasted_iota(jnp.int32, (RET_CHUNK, LANES), 1)
    for h in range(RET_HEADS):
        pair, half = divmod(h, 2)
        q = q_ref[pair]
        k = k_ref[pair]
        v = v_ref[h]
        own = (lane >= RET_DK) if half else (lane < RET_DK)
        qm = jnp.where(own, q, jnp.zeros_like(q))
        inner = (_dot_nt(qm, k) * decay_ref[h]).astype(BF16)
        state = state_ref[h]
        o = _dot(inner, v) + _dot(qm, state.astype(BF16)) * xi_ref[h]
        kz = (k.astype(F32) * zeta_ref[h]).T.astype(BF16)
        state_ref[h] = g_chunk[h] * state + _dot(kz, v)
        y = _rms(o, g_ref[h:h + 1, :]) * gate_ref[h].astype(F32)
        o_ref[:, h * RET_DV:(h + 1) * RET_DV] = y.astype(BF16)


def _retention_consts():
    c = RET_CHUNK
    log_g = np.log(1.0 - 2.0 ** (-5.0 - np.arange(RET_HEADS, dtype=np.float64)))
    n = np.arange(c, dtype=np.float64)
    rel = n[:, None] - n[None, :]
    decay = np.where(rel[None] >= 0, np.exp(np.maximum(rel, 0.0)[None] * log_g[:, None, None]), 0.0)
    xi = np.exp((n[None, :] + 1.0) * log_g[:, None])[:, :, None]
    zeta = np.exp((c - 1.0 - n[None, :]) * log_g[:, None])[:, :, None]
    g_chunk = tuple(float(v) for v in np.exp(c * log_g))
    as32 = lambda a: jnp.asarray(a, dtype=F32)
    return as32(decay), as32(xi), as32(zeta), g_chunk


def _retention(proj, ret_norm, batch, seq):
    t = batch * seq
    nc = seq // RET_CHUNK
    decay, xi, zeta, g_chunk = _retention_consts()
    row = lambda b, c: b * nc + c
    return pl.pallas_call(
        functools.partial(_retention_kernel, g_chunk=g_chunk),
        grid=(batch, nc),
        in_specs=[
            pl.BlockSpec((2, RET_CHUNK, LANES), lambda b, c: (0, row(b, c), 0)),
            pl.BlockSpec((2, RET_CHUNK, LANES), lambda b, c: (1, row(b, c), 0)),
            pl.BlockSpec((4, RET_CHUNK, LANES), lambda b, c: (1, row(b, c), 0)),
            pl.BlockSpec((4, RET_CHUNK, LANES), lambda b, c: (2, row(b, c), 0)),
            _resident(decay.shape), _resident(xi.shape), _resident(zeta.shape),
            _resident(ret_norm.shape),
        ],
        out_specs=pl.BlockSpec((RET_CHUNK, RET_HEADS * RET_DV), lambda b, c: (row(b, c), 0)),
        out_shape=jax.ShapeDtypeStruct((t, RET_HEADS * RET_DV), BF16),
        scratch_shapes=[pltpu.VMEM((RET_HEADS, LANES, RET_DV), F32)],
        compiler_params=_params("parallel", "arbitrary"),
        name="retention",
    )(proj, proj, proj, proj, decay, xi, zeta, ret_norm)


def _diff_kernel(lq1_ref, lk1_ref, lq2_ref, lk2_ref, g_ref, q_ref, k_ref, v_ref,
                 o_ref, m_ref, l_ref, acc_ref):
    i = pl.program_id(2)
    q = q_ref[0]
    lane = lax.broadcasted_iota(jnp.int32, q.shape, 1)
    zero = jnp.zeros_like(q)
    qs = (jnp.where(lane < DIFF_DH, q, zero), jnp.where(lane >= DIFF_DH, q, zero))

    m_ref[...] = jnp.full_like(m_ref, -jnp.inf)
    l_ref[...] = jnp.zeros_like(l_ref)
    acc_ref[...] = jnp.zeros_like(acc_ref)

    def step(j, masked):
        start = pl.multiple_of(j * ATT_TILE, ATT_TILE)
        k = k_ref[0, pl.ds(start, ATT_TILE), :]
        v = v_ref[0, pl.ds(start, ATT_TILE), :]
        for c in range(2):
            s = _dot_nt(qs[c], k)
            if masked:
                row = lax.broadcasted_iota(jnp.int32, s.shape, 0)
                col = lax.broadcasted_iota(jnp.int32, s.shape, 1)
                s = jnp.where(col <= row, s, -jnp.inf)
            m_prev = m_ref[c]
            m_new = jnp.maximum(m_prev, jnp.max(s, axis=-1, keepdims=True))
            alpha = jnp.exp(m_prev - m_new)
            p = jnp.exp(s - m_new)
            l_ref[c] = alpha * l_ref[c] + jnp.sum(p, axis=-1, keepdims=True)
            acc_ref[c] = alpha * acc_ref[c] + _dot(p.astype(BF16), v)
            m_ref[c] = m_new

    def body(j, carry):
        step(j, False)
        return carry

    lax.fori_loop(0, i, body, 0)
    step(i, True)

    lam = (jnp.exp(jnp.sum(lq1_ref[...] * lk1_ref[...], keepdims=True))
           - jnp.exp(jnp.sum(lq2_ref[...] * lk2_ref[...], keepdims=True)) + LAMBDA_INIT)
    out = acc_ref[0] / l_ref[0] - lam * (acc_ref[1] / l_ref[1])
    o_ref[...] = (_rms(out, g_ref[0]) * (1.0 - LAMBDA_INIT)).astype(BF16)


def _diffattn(proj, lq1, lk1, lq2, lk2, gain, batch, seq):
    t = batch * seq
    nq = seq // ATT_TILE
    vec = pl.BlockSpec((1, DIFF_DH), lambda b, h, i: (0, 0))
    return pl.pallas_call(
        _diff_kernel,
        grid=(batch, DIFF_HEADS, nq),
        in_specs=[
            vec, vec, vec, vec,
            pl.BlockSpec((1, 1, DIFF_DV), lambda b, h, i: (h, 0, 0)),
            pl.BlockSpec((1, ATT_TILE, LANES), lambda b, h, i: (BLK_DQ + h, b * nq + i, 0)),
            pl.BlockSpec((1, seq, LANES), lambda b, h, i: (BLK_DK + h, b, 0)),
            pl.BlockSpec((1, seq, LANES), lambda b, h, i: (BLK_DV + h, b, 0)),
        ],
        out_specs=pl.BlockSpec((ATT_TILE, DIFF_DV), lambda b, h, i: (b * nq + i, h)),
        out_shape=jax.ShapeDtypeStruct((t, DIFF_HEADS * DIFF_DV), BF16),
        scratch_shapes=[pltpu.VMEM((2, ATT_TILE, 1), F32), pltpu.VMEM((2, ATT_TILE, 1), F32),
                        pltpu.VMEM((2, ATT_TILE, DIFF_DV), F32)],
        compiler_params=_params("parallel", "parallel", "parallel"),
        name="diffattn",
    )(lq1, lk1, lq2, lk2, gain, proj, proj, proj)


def _kv_kernel(m_ref, g_ref, w_ref, o_ref):
    o_ref[...] = _dot(_rms(m_ref[...], g_ref[...]).astype(BF16), w_ref[...]).astype(BF16)


def _kvproj(mem, g, w):
    t, d = mem.shape
    return pl.pallas_call(
        _kv_kernel,
        grid=(t // TOKEN_TILE,),
        in_specs=[pl.BlockSpec((TOKEN_TILE, d), lambda i: (i, 0)), _resident(g.shape),
                  _resident(w.shape)],
        out_specs=pl.BlockSpec((TOKEN_TILE, w.shape[1]), lambda i: (i, 0)),
        out_shape=jax.ShapeDtypeStruct((t, w.shape[1]), BF16),
        compiler_params=_params("parallel"),
        name="kvproj",
    )(mem, g, w)


def _mid_kernel(x_ref, yr_ref, yd_ref, wout_ref, g_ref, wq_ref, kv_ref, wo_ref, o_ref):
    half = wout_ref.shape[0] // 2
    x = x_ref[...] + _dot(yr_ref[...], wout_ref[0:half, :]) + _dot(yd_ref[...], wout_ref[half:, :])
    h = _rms(x, g_ref[...]).astype(BF16)
    q = (_dot(h, wq_ref[...]) * (XATTN_DH ** -0.5)).astype(BF16)
    outs = []
    for hd in range(XATTN_HEADS):
        cols = slice(hd * XATTN_DH, (hd + 1) * XATTN_DH)
        k = kv_ref[:, cols]
        v = kv_ref[:, D_MODEL + hd * XATTN_DH:D_MODEL + (hd + 1) * XATTN_DH]
        s = _dot_nt(q[:, cols], k)
        p = jnp.exp(s - jnp.max(s, axis=-1, keepdims=True))
        o = _dot(p.astype(BF16), v) / jnp.sum(p, axis=-1, keepdims=True)
        outs.append(o.astype(BF16))
    o_ref[...] = x + _dot(jnp.concatenate(outs, axis=-1), wo_ref[...])


def _mid(x, yr, yd, wout, g, wq, kv, wo, seq):
    t, d = x.shape
    steps_per_seq = seq // TOKEN_TILE
    tile = lambda w: pl.BlockSpec((TOKEN_TILE, w), lambda i: (i, 0))
    return pl.pallas_call(
        _mid_kernel,
        grid=(t // TOKEN_TILE,),
        in_specs=[tile(d), tile(yr.shape[1]), tile(yd.shape[1]), _resident(wout.shape),
                  _resident(g.shape), _resident(wq.shape),
                  pl.BlockSpec((N_MEM, kv.shape[1]), lambda i: (i // steps_per_seq, 0)),
                  _resident(wo.shape)],
        out_specs=tile(d),
        out_shape=jax.ShapeDtypeStruct((t, d), F32),
        compiler_params=_params("parallel"),
        name="mid",
    )(x, yr, yd, wout, g, wq, kv, wo)


def kernel(x, mem, ffn1_norm, ffn1_w_gate, ffn1_w_up, ffn1_w_down, mix_norm, w_in, ret_out_norm, diff_lq1, diff_lk1, diff_lq2, diff_lk2, diff_out_norm, w_out, xattn_norm, mem_norm, xattn_wq, xattn_wkv, xattn_wo, ffn2_norm, ffn2_w_gate, ffn2_w_up, ffn2_w_down, final_norm):
    batch, seq, d = x.shape
    assert ffn1_norm.shape[0] == 1, "single-layer kernel"
    assert seq % ATT_TILE == 0 and seq % RET_CHUNK == 0 and seq % TOKEN_TILE == 0
    row = lambda v: v.reshape(1, -1).astype(F32)
    w16 = lambda w: w[0].astype(BF16)

    xt = x.reshape(batch * seq, d)
    final_g = row(final_norm)

    x1 = _ffn(xt, row(ffn1_norm[0]), w16(ffn1_w_gate), w16(ffn1_w_up), w16(ffn1_w_down),
              final_g, final=False)
    proj = _inproj(x1, row(mix_norm[0]), w16(w_in), seq)
    y_ret = _retention(proj, ret_out_norm[0].astype(F32), batch, seq)
    y_diff = _diffattn(proj, row(diff_lq1[0]), row(diff_lk1[0]), row(diff_lq2[0]), row(diff_lk2[0]),
                       diff_out_norm[0].reshape(DIFF_HEADS, 1, DIFF_DV).astype(F32), batch, seq)
    kv = _kvproj(mem.reshape(batch * N_MEM, d), row(mem_norm[0]), w16(xattn_wkv))
    x3 = _mid(x1, y_ret, y_diff, w16(w_out), row(xattn_norm[0]), w16(xattn_wq), kv,
              w16(xattn_wo), seq)
    y = _ffn(x3, row(ffn2_norm[0]), w16(ffn2_w_gate), w16(ffn2_w_up), w16(ffn2_w_down),
             final_g, final=True)
    return y.reshape(batch, seq, d)
```

```python
import functools
import math

import jax
import jax.numpy as jnp
import numpy as np
from jax import lax
from jax.experimental import pallas as pl
from jax.experimental.pallas import tpu as pltpu

F32 = jnp.float32
BF16 = jnp.bfloat16

D_MODEL = 1024
N_MEM = 256
RET_HEADS = 4
RET_DK = 64
RET_DV = 128
DIFF_HEADS = 4
DIFF_DH = 64
DIFF_DV = 128
IN_COLS = 3072
XATTN_HEADS = 4
XATTN_DH = 256
ROPE_BASE = 10000.0
EPS = 1e-6
LAMBDA_INIT = 0.8 - 0.6 * math.exp(-0.3 * 0)

LANES = 128
BLK_DQ, BLK_DK, N_COL_BLOCKS = 12, 16, 20
DV_COL0 = N_COL_BLOCKS * LANES

TOKEN_TILE = 512
RET_CHUNK = 256
ATT_TILE = 512
VMEM_LIMIT = 56 * 1024 * 1024


def _rms(x, g):
    ms = jnp.mean(x * x, axis=-1, keepdims=True)
    return x * lax.rsqrt(ms + EPS) * g


def _dot(a, b):
    return jnp.dot(a, b, preferred_element_type=F32)


def _dot_nt(a, b):
    return lax.dot_general(a, b, (((1,), (1,)), ((), ())), preferred_element_type=F32)


def _resident(shape):
    nd = len(shape)
    return pl.BlockSpec(shape, lambda *_: (0,) * nd, pipeline_mode=pl.Buffered(1))


def _params(*sem):
    return pltpu.CompilerParams(dimension_semantics=sem, vmem_limit_bytes=VMEM_LIMIT)


def _ffn_kernel(x_ref, g_ref, wg_ref, wu_ref, wd_ref, gf_ref, o_ref, *, final):
    x = x_ref[...]
    h = _rms(x, g_ref[...]).astype(BF16)
    a = _dot(h, wg_ref[...])
    u = _dot(h, wu_ref[...])
    act = (a * jax.nn.sigmoid(a) * u).astype(BF16)
    y = x + 0.5 * _dot(act, wd_ref[...])
    if final:
        y = _rms(y, gf_ref[...])
    o_ref[...] = y


def _ffn(x, g, wg, wu, wd, gf, final):
    t, d = x.shape
    tile = pl.BlockSpec((TOKEN_TILE, d), lambda i: (i, 0))
    return pl.pallas_call(
        functools.partial(_ffn_kernel, final=final),
        grid=(t // TOKEN_TILE,),
        in_specs=[tile, _resident(g.shape), _resident(wg.shape), _resident(wu.shape),
                  _resident(wd.shape), _resident(gf.shape)],
        out_specs=tile,
        out_shape=jax.ShapeDtypeStruct((t, d), F32),
        compiler_params=_params("parallel"),
        name="ffn_final" if final else "ffn",
    )(x, g, wg, wu, wd, gf)


def _inproj_kernel(x_ref, g_ref, w_ref, cos_ref, sa_ref, sb_ref, o_ref, vt_ref):
    h = _rms(x_ref[...], g_ref[...]).astype(BF16)
    p = _dot(h, w_ref[...])
    cos, sa, sb = cos_ref[...], sa_ref[...], sb_ref[...]
    qk_w = RET_HEADS * RET_DK

    def rot(t):
        up = pltpu.roll(t, qk_w - RET_DK // 2, 1)
        down = pltpu.roll(t, RET_DK // 2, 1)
        return t * cos + up * sa + down * sb

    rq = rot(p[:, 0:qk_w])
    rk = rot(p[:, qk_w:2 * qk_w]) * (RET_DK ** -0.5)
    for j in range(2):
        o_ref[j] = rq[:, j * LANES:(j + 1) * LANES].astype(BF16)
        o_ref[2 + j] = rk[:, j * LANES:(j + 1) * LANES].astype(BF16)
    for j in range(4, 8):
        o_ref[j] = p[:, j * LANES:(j + 1) * LANES].astype(BF16)
    for j in range(8, 12):
        gcol = p[:, j * LANES:(j + 1) * LANES]
        o_ref[j] = (gcol * jax.nn.sigmoid(gcol)).astype(BF16)
    for j in range(BLK_DQ, BLK_DK):
        o_ref[j] = (p[:, j * LANES:(j + 1) * LANES] * (DIFF_DH ** -0.5)).astype(BF16)
    for j in range(BLK_DK, N_COL_BLOCKS):
        o_ref[j] = p[:, j * LANES:(j + 1) * LANES].astype(BF16)
    vt_ref[...] = p[:, DV_COL0:].T.astype(BF16)


def _rope_tables(seq):
    half = RET_DK // 2
    inv = 1.0 / (ROPE_BASE ** (jnp.arange(0, RET_DK, 2, dtype=F32) / RET_DK))
    ang = jnp.arange(seq, dtype=F32)[:, None] * inv[None, :]
    cos, sin = jnp.cos(ang), jnp.sin(ang)
    zero = jnp.zeros_like(sin)
    tile = lambda a, b: jnp.tile(jnp.concatenate([a, b], axis=-1), (1, RET_HEADS))
    assert half * 2 == RET_DK
    return tile(cos, cos), tile(-sin, zero), tile(zero, sin)


def _inproj(x, g, w, seq):
    t, d = x.shape
    steps_per_seq = seq // TOKEN_TILE
    tables = _rope_tables(seq)
    tab_spec = pl.BlockSpec((TOKEN_TILE, RET_HEADS * RET_DK), lambda i: (i % steps_per_seq, 0))
    return pl.pallas_call(
        _inproj_kernel,
        grid=(t // TOKEN_TILE,),
        in_specs=[pl.BlockSpec((TOKEN_TILE, d), lambda i: (i, 0)), _resident(g.shape),
                  _resident(w.shape), tab_spec, tab_spec, tab_spec],
        out_specs=[pl.BlockSpec((N_COL_BLOCKS, TOKEN_TILE, LANES), lambda i: (0, i, 0)),
                   pl.BlockSpec((DIFF_HEADS * DIFF_DV, TOKEN_TILE), lambda i: (0, i))],
        out_shape=[jax.ShapeDtypeStruct((N_COL_BLOCKS, t, LANES), BF16),
                   jax.ShapeDtypeStruct((DIFF_HEADS * DIFF_DV, t), BF16)],
        compiler_params=_params("parallel"),
        name="inproj",
    )(x, g, w, *tables)


def _retention_kernel(q_ref, k_ref, v_ref, gate_ref, decay_ref, xi_ref, zeta_ref, g_ref,
                      o_ref, state_ref, *, g_chunk):
    @pl.when(pl.program_id(1) == 0)
    def _():
        state_ref[...] = jnp.zeros_like(state_ref)

    lane = lax.broadcasted_iota(jnp.int32, (RET_CHUNK, LANES), 1)
    for h in range(RET_HEADS):
        pair, half = divmod(h, 2)
        q = q_ref[pair]
        k = k_ref[pair]
        v = v_ref[h]
        own = (lane >= RET_DK) if half else (lane < RET_DK)
        qm = jnp.where(own, q, jnp.zeros_like(q))
        inner = (_dot_nt(qm, k) * decay_ref[h]).astype(BF16)
        state = state_ref[h]
        o = _dot(inner, v) + _dot(qm, state.astype(BF16)) * xi_ref[h]
        kz = (k.astype(F32) * zeta_ref[h]).T.astype(BF16)
        state_ref[h] = g_chunk[h] * state + _dot(kz, v)
        y = _rms(o, g_ref[h:h + 1, :]) * gate_ref[h].astype(F32)
        o_ref[:, h * RET_DV:(h + 1) * RET_DV] = y.astype(BF16)


def _retention_consts():
    c = RET_CHUNK
    log_g = np.log(1.0 - 2.0 ** (-5.0 - np.arange(RET_HEADS, dtype=np.float64)))
    n = np.arange(c, dtype=np.float64)
    rel = n[:, None] - n[None, :]
    decay = np.where(rel[None] >= 0, np.exp(np.maximum(rel, 0.0)[None] * log_g[:, None, None]), 0.0)
    xi = np.exp((n[None, :] + 1.0) * log_g[:, None])[:, :, None]
    zeta = np.exp((c - 1.0 - n[None, :]) * log_g[:, None])[:, :, None]
    g_chunk = tuple(float(v) for v in np.exp(c * log_g))
    as32 = lambda a: jnp.asarray(a, dtype=F32)
    return as32(decay), as32(xi), as32(zeta), g_chunk


def _retention(proj, ret_norm, batch, seq):
    t = batch * seq
    nc = seq // RET_CHUNK
    decay, xi, zeta, g_chunk = _retention_consts()
    row = lambda b, c: b * nc + c
    return pl.pallas_call(
        functools.partial(_retention_kernel, g_chunk=g_chunk),
        grid=(batch, nc),
        in_specs=[
            pl.BlockSpec((2, RET_CHUNK, LANES), lambda b, c: (0, row(b, c), 0)),
            pl.BlockSpec((2, RET_CHUNK, LANES), lambda b, c: (1, row(b, c), 0)),
            pl.BlockSpec((4, RET_CHUNK, LANES), lambda b, c: (1, row(b, c), 0)),
            pl.BlockSpec((4, RET_CHUNK, LANES), lambda b, c: (2, row(b, c), 0)),
            _resident(decay.shape), _resident(xi.shape), _resident(zeta.shape),
            _resident(ret_norm.shape),
        ],
        out_specs=pl.BlockSpec((RET_CHUNK, RET_HEADS * RET_DV), lambda b, c: (row(b, c), 0)),
        out_shape=jax.ShapeDtypeStruct((t, RET_HEADS * RET_DV), BF16),
        scratch_shapes=[pltpu.VMEM((RET_HEADS, LANES, RET_DV), F32)],
        compiler_params=_params("parallel", "arbitrary"),
        name="retention",
    )(proj, proj, proj, proj, decay, xi, zeta, ret_norm)


def _diff_kernel(lq1_ref, lk1_ref, lq2_ref, lk2_ref, g_ref, q_ref, k_ref, vt_ref,
                 o_ref, m_ref, l_ref, acc_ref):
    i = pl.program_id(2)
    q = q_ref[0]
    lane = lax.broadcasted_iota(jnp.int32, q.shape, 1)
    zero = jnp.zeros_like(q)
    qs = (jnp.where(lane < DIFF_DH, q, zero), jnp.where(lane >= DIFF_DH, q, zero))

    m_ref[...] = jnp.full_like(m_ref, -jnp.inf)
    l_ref[...] = jnp.zeros_like(l_ref)
    acc_ref[...] = jnp.zeros_like(acc_ref)

    def step(j, masked):
        start = pl.multiple_of(j * ATT_TILE, ATT_TILE)
        k = k_ref[0, pl.ds(start, ATT_TILE), :]
        vt = vt_ref[:, pl.ds(start, ATT_TILE)]
        for c in range(2):
            st = _dot_nt(k, qs[c])
            if masked:
                key = lax.broadcasted_iota(jnp.int32, st.shape, 0)
                qry = lax.broadcasted_iota(jnp.int32, st.shape, 1)
                st = jnp.where(key <= qry, st, -jnp.inf)
            m_prev = m_ref[c]
            m_new = jnp.maximum(m_prev, jnp.max(st, axis=0, keepdims=True))
            alpha = jnp.exp(m_prev - m_new)
            p = jnp.exp(st - m_new)
            l_ref[c] = alpha * l_ref[c] + jnp.sum(p, axis=0, keepdims=True)
            acc_ref[c] = alpha * acc_ref[c] + _dot(vt, p.astype(BF16))
            m_ref[c] = m_new

    def body(j, carry):
        step(j, False)
        return carry

    lax.fori_loop(0, i, body, 0)
    step(i, True)

    lam = (jnp.exp(jnp.sum(lq1_ref[...] * lk1_ref[...], keepdims=True))
           - jnp.exp(jnp.sum(lq2_ref[...] * lk2_ref[...], keepdims=True)) + LAMBDA_INIT)
    out_t = acc_ref[0] / l_ref[0] - lam * (acc_ref[1] / l_ref[1])
    ms = jnp.mean(out_t * out_t, axis=0, keepdims=True)
    y = (out_t * lax.rsqrt(ms + EPS)).T * g_ref[0]
    o_ref[...] = (y * (1.0 - LAMBDA_INIT)).astype(BF16)


def _diffattn(proj, vt, lq1, lk1, lq2, lk2, gain, batch, seq):
    t = batch * seq
    nq = seq // ATT_TILE
    vec = pl.BlockSpec((1, DIFF_DH), lambda b, h, i: (0, 0))
    return pl.pallas_call(
        _diff_kernel,
        grid=(batch, DIFF_HEADS, nq),
        in_specs=[
            vec, vec, vec, vec,
            pl.BlockSpec((1, 1, DIFF_DV), lambda b, h, i: (h, 0, 0)),
            pl.BlockSpec((1, ATT_TILE, LANES), lambda b, h, i: (BLK_DQ + h, b * nq + i, 0)),
            pl.BlockSpec((1, seq, LANES), lambda b, h, i: (BLK_DK + h, b, 0)),
            pl.BlockSpec((DIFF_DV, seq), lambda b, h, i: (h, b)),
        ],
        out_specs=pl.BlockSpec((ATT_TILE, DIFF_DV), lambda b, h, i: (b * nq + i, h)),
        out_shape=jax.ShapeDtypeStruct((t, DIFF_HEADS * DIFF_DV), BF16),
        scratch_shapes=[pltpu.VMEM((2, 1, ATT_TILE), F32), pltpu.VMEM((2, 1, ATT_TILE), F32),
                        pltpu.VMEM((2, DIFF_DV, ATT_TILE), F32)],
        compiler_params=_params("parallel", "parallel", "parallel"),
        name="diffattn",
    )(lq1, lk1, lq2, lk2, gain, proj, proj, vt)


def _kv_kernel(m_ref, g_ref, w_ref, o_ref):
    o_ref[...] = _dot(_rms(m_ref[...], g_ref[...]).astype(BF16), w_ref[...]).astype(BF16)


def _kvproj(mem, g, w):
    t, d = mem.shape
    return pl.pallas_call(
        _kv_kernel,
        grid=(t // TOKEN_TILE,),
        in_specs=[pl.BlockSpec((TOKEN_TILE, d), lambda i: (i, 0)), _resident(g.shape),
                  _resident(w.shape)],
        out_specs=pl.BlockSpec((TOKEN_TILE, w.shape[1]), lambda i: (i, 0)),
        out_shape=jax.ShapeDtypeStruct((t, w.shape[1]), BF16),
        compiler_params=_params("parallel"),
        name="kvproj",
    )(mem, g, w)


def _mid_kernel(x_ref, yr_ref, yd_ref, wout_ref, g_ref, wq_ref, kv_ref, wo_ref, o_ref):
    half = wout_ref.shape[0] // 2
    x = x_ref[...] + _dot(yr_ref[...], wout_ref[0:half, :]) + _dot(yd_ref[...], wout_ref[half:, :])
    h = _rms(x, g_ref[...]).astype(BF16)
    q = (_dot(h, wq_ref[...]) * (XATTN_DH ** -0.5)).astype(BF16)
    outs = []
    for hd in range(XATTN_HEADS):
        cols = slice(hd * XATTN_DH, (hd + 1) * XATTN_DH)
        k = kv_ref[:, cols]
        v = kv_ref[:, D_MODEL + hd * XATTN_DH:D_MODEL + (hd + 1) * XATTN_DH]
        s = _dot_nt(q[:, cols], k)
        p = jnp.exp(s - jnp.max(s, axis=-1, keepdims=True))
        o = _dot(p.astype(BF16), v) / jnp.sum(p, axis=-1, keepdims=True)
        outs.append(o.astype(BF16))
    o_ref[...] = x + _dot(jnp.concatenate(outs, axis=-1), wo_ref[...])


def _mid(x, yr, yd, wout, g, wq, kv, wo, seq):
    t, d = x.shape
    steps_per_seq = seq // TOKEN_TILE
    tile = lambda w: pl.BlockSpec((TOKEN_TILE, w), lambda i: (i, 0))
    return pl.pallas_call(
        _mid_kernel,
        grid=(t // TOKEN_TILE,),
        in_specs=[tile(d), tile(yr.shape[1]), tile(yd.shape[1]), _resident(wout.shape),
                  _resident(g.shape), _resident(wq.shape),
                  pl.BlockSpec((N_MEM, kv.shape[1]), lambda i: (i // steps_per_seq, 0)),
                  _resident(wo.shape)],
        out_specs=tile(d),
        out_shape=jax.ShapeDtypeStruct((t, d), F32),
        compiler_params=_params("parallel"),
        name="mid",
    )(x, yr, yd, wout, g, wq, kv, wo)


def kernel(x, mem, ffn1_norm, ffn1_w_gate, ffn1_w_up, ffn1_w_down, mix_norm, w_in, ret_out_norm, diff_lq1, diff_lk1, diff_lq2, diff_lk2, diff_out_norm, w_out, xattn_norm, mem_norm, xattn_wq, xattn_wkv, xattn_wo, ffn2_norm, ffn2_w_gate, ffn2_w_up, ffn2_w_down, final_norm):
    batch, seq, d = x.shape
    assert ffn1_norm.shape[0] == 1, "single-layer kernel"
    assert seq % ATT_TILE == 0 and seq % RET_CHUNK == 0 and seq % TOKEN_TILE == 0
    row = lambda v: v.reshape(1, -1).astype(F32)
    w16 = lambda w: w[0].astype(BF16)

    xt = x.reshape(batch * seq, d)
    final_g = row(final_norm)

    x1 = _ffn(xt, row(ffn1_norm[0]), w16(ffn1_w_gate), w16(ffn1_w_up), w16(ffn1_w_down),
              final_g, final=False)
    proj, vt = _inproj(x1, row(mix_norm[0]), w16(w_in), seq)
    y_ret = _retention(proj, ret_out_norm[0].astype(F32), batch, seq)
    y_diff = _diffattn(proj, vt, row(diff_lq1[0]), row(diff_lk1[0]), row(diff_lq2[0]), row(diff_lk2[0]),
                       diff_out_norm[0].reshape(DIFF_HEADS, 1, DIFF_DV).astype(F32), batch, seq)
    kv = _kvproj(mem.reshape(batch * N_MEM, d), row(mem_norm[0]), w16(xattn_wkv))
    x3 = _mid(x1, y_ret, y_diff, w16(w_out), row(xattn_norm[0]), w16(xattn_wq), kv,
              w16(xattn_wo), seq)
    y = _ffn(x3, row(ffn2_norm[0]), w16(ffn2_w_gate), w16(ffn2_w_up), w16(ffn2_w_down),
             final_g, final=True)
    return y.reshape(batch, seq, d)
```

```python
import functools
import math

import jax
import jax.numpy as jnp
import numpy as np
from jax import lax
from jax.experimental import pallas as pl
from jax.experimental.pallas import tpu as pltpu

F32 = jnp.float32
BF16 = jnp.bfloat16

D_MODEL = 1024
N_MEM = 256
RET_HEADS = 4
RET_DK = 64
RET_DV = 128
DIFF_HEADS = 4
DIFF_DH = 64
DIFF_DV = 128
IN_COLS = 3072
XATTN_HEADS = 4
XATTN_DH = 256
ROPE_BASE = 10000.0
EPS = 1e-6
LAMBDA_INIT = 0.8 - 0.6 * math.exp(-0.3 * 0)
LOG2E = math.log2(math.e)

LANES = 128
BLK_DQ, BLK_DK, N_COL_BLOCKS = 12, 16, 20
DV_COL0 = N_COL_BLOCKS * LANES

TOKEN_TILE = 512
RET_CHUNK = 256
ATT_TILE = 512
VMEM_LIMIT = 56 * 1024 * 1024


def _rms(x, g):
    ms = jnp.mean(x * x, axis=-1, keepdims=True)
    return x * lax.rsqrt(ms + EPS) * g


def _dot(a, b):
    return jnp.dot(a, b, preferred_element_type=F32)


def _dot_nt(a, b):
    return lax.dot_general(a, b, (((1,), (1,)), ((), ())), preferred_element_type=F32)


def _resident(shape):
    nd = len(shape)
    return pl.BlockSpec(shape, lambda *_: (0,) * nd, pipeline_mode=pl.Buffered(1))


def _params(*sem):
    return pltpu.CompilerParams(dimension_semantics=sem, vmem_limit_bytes=VMEM_LIMIT)


def _ffn_kernel(x_ref, g_ref, wg_ref, wu_ref, wd_ref, gf_ref, o_ref, *, final):
    x = x_ref[...]
    h = _rms(x, g_ref[...]).astype(BF16)
    a = _dot(h, wg_ref[...])
    u = _dot(h, wu_ref[...])
    act = (a * jax.nn.sigmoid(a) * u).astype(BF16)
    y = x + 0.5 * _dot(act, wd_ref[...])
    if final:
        y = _rms(y, gf_ref[...])
    o_ref[...] = y


def _ffn(x, g, wg, wu, wd, gf, final):
    t, d = x.shape
    tile = pl.BlockSpec((TOKEN_TILE, d), lambda i: (i, 0))
    return pl.pallas_call(
        functools.partial(_ffn_kernel, final=final),
        grid=(t // TOKEN_TILE,),
        in_specs=[tile, _resident(g.shape), _resident(wg.shape), _resident(wu.shape),
                  _resident(wd.shape), _resident(gf.shape)],
        out_specs=tile,
        out_shape=jax.ShapeDtypeStruct((t, d), F32),
        compiler_params=_params("parallel"),
        name="ffn_final" if final else "ffn",
    )(x, g, wg, wu, wd, gf)


def _inproj_kernel(x_ref, g_ref, w_ref, cos_ref, sa_ref, sb_ref, o_ref, vt_ref):
    h = _rms(x_ref[...], g_ref[...]).astype(BF16)
    p = _dot(h, w_ref[...])
    cos, sa, sb = cos_ref[...], sa_ref[...], sb_ref[...]
    qk_w = RET_HEADS * RET_DK

    def rot(t):
        up = pltpu.roll(t, qk_w - RET_DK // 2, 1)
        down = pltpu.roll(t, RET_DK // 2, 1)
        return t * cos + up * sa + down * sb

    rq = rot(p[:, 0:qk_w])
    rk = rot(p[:, qk_w:2 * qk_w]) * (RET_DK ** -0.5)
    for j in range(2):
        o_ref[j] = rq[:, j * LANES:(j + 1) * LANES].astype(BF16)
        o_ref[2 + j] = rk[:, j * LANES:(j + 1) * LANES].astype(BF16)
    for j in range(4, 8):
        o_ref[j] = p[:, j * LANES:(j + 1) * LANES].astype(BF16)
    for j in range(8, 12):
        gcol = p[:, j * LANES:(j + 1) * LANES]
        o_ref[j] = (gcol * jax.nn.sigmoid(gcol)).astype(BF16)
    for j in range(BLK_DQ, BLK_DK):
        o_ref[j] = (p[:, j * LANES:(j + 1) * LANES] * (DIFF_DH ** -0.5 * LOG2E)).astype(BF16)
    for j in range(BLK_DK, N_COL_BLOCKS):
        o_ref[j] = p[:, j * LANES:(j + 1) * LANES].astype(BF16)
    vt_ref[...] = p[:, DV_COL0:].T.astype(BF16)


def _rope_tables(seq):
    half = RET_DK // 2
    inv = 1.0 / (ROPE_BASE ** (jnp.arange(0, RET_DK, 2, dtype=F32) / RET_DK))
    ang = jnp.arange(seq, dtype=F32)[:, None] * inv[None, :]
    cos, sin = jnp.cos(ang), jnp.sin(ang)
    zero = jnp.zeros_like(sin)
    tile = lambda a, b: jnp.tile(jnp.concatenate([a, b], axis=-1), (1, RET_HEADS))
    assert half * 2 == RET_DK
    return tile(cos, cos), tile(-sin, zero), tile(zero, sin)


def _inproj(x, g, w, seq):
    t, d = x.shape
    steps_per_seq = seq // TOKEN_TILE
    tables = _rope_tables(seq)
    tab_spec = pl.BlockSpec((TOKEN_TILE, RET_HEADS * RET_DK), lambda i: (i % steps_per_seq, 0))
    return pl.pallas_call(
        _inproj_kernel,
        grid=(t // TOKEN_TILE,),
        in_specs=[pl.BlockSpec((TOKEN_TILE, d), lambda i: (i, 0)), _resident(g.shape),
                  _resident(w.shape), tab_spec, tab_spec, tab_spec],
        out_specs=[pl.BlockSpec((N_COL_BLOCKS, TOKEN_TILE, LANES), lambda i: (0, i, 0)),
                   pl.BlockSpec((DIFF_HEADS * DIFF_DV, TOKEN_TILE), lambda i: (0, i))],
        out_shape=[jax.ShapeDtypeStruct((N_COL_BLOCKS, t, LANES), BF16),
                   jax.ShapeDtypeStruct((DIFF_HEADS * DIFF_DV, t), BF16)],
        compiler_params=_params("parallel"),
        name="inproj",
    )(x, g, w, *tables)


def _retention_kernel(q_ref, k_ref, v_ref, gate_ref, decay_ref, xi_ref, zeta_ref, g_ref,
                      o_ref, state_ref, *, g_chunk):
    @pl.when(pl.program_id(1) == 0)
    def _():
        state_ref[...] = jnp.zeros_like(state_ref)

    lane = lax.broadcasted_iota(jnp.int32, (RET_CHUNK, LANES), 1)
    for h in range(RET_HEADS):
        pair, half = divmod(h, 2)
        q = q_ref[pair]
        k = k_ref[pair]
        v = v_ref[h]
        own = (lane >= RET_DK) if half else (lane < RET_DK)
        qm = jnp.where(own, q, jnp.zeros_like(q))
        inner = (_dot_nt(qm, k) * decay_ref[h]).astype(BF16)
        state = state_ref[h]
        o = _dot(inner, v) + _dot(qm, state.astype(BF16)) * xi_ref[h]
        kz = (k.astype(F32) * zeta_ref[h]).T.astype(BF16)
        state_ref[h] = g_chunk[h] * state + _dot(kz, v)
        y = _rms(o, g_ref[h:h + 1, :]) * gate_ref[h].astype(F32)
        o_ref[:, h * RET_DV:(h + 1) * RET_DV] = y.astype(BF16)


def _retention_consts():
    c = RET_CHUNK
    log_g = np.log(1.0 - 2.0 ** (-5.0 - np.arange(RET_HEADS, dtype=np.float64)))
    n = np.arange(c, dtype=np.float64)
    rel = n[:, None] - n[None, :]
    decay = np.where(rel[None] >= 0, np.exp(np.maximum(rel, 0.0)[None] * log_g[:, None, None]), 0.0)
    xi = np.exp((n[None, :] + 1.0) * log_g[:, None])[:, :, None]
    zeta = np.exp((c - 1.0 - n[None, :]) * log_g[:, None])[:, :, None]
    g_chunk = tuple(float(v) for v in np.exp(c * log_g))
    as32 = lambda a: jnp.asarray(a, dtype=F32)
    return as32(decay), as32(xi), as32(zeta), g_chunk


def _retention(proj, ret_norm, batch, seq):
    t = batch * seq
    nc = seq // RET_CHUNK
    decay, xi, zeta, g_chunk = _retention_consts()
    row = lambda b, c: b * nc + c
    return pl.pallas_call(
        functools.partial(_retention_kernel, g_chunk=g_chunk),
        grid=(batch, nc),
        in_specs=[
            pl.BlockSpec((2, RET_CHUNK, LANES), lambda b, c: (0, row(b, c), 0)),
            pl.BlockSpec((2, RET_CHUNK, LANES), lambda b, c: (1, row(b, c), 0)),
            pl.BlockSpec((4, RET_CHUNK, LANES), lambda b, c: (1, row(b, c), 0)),
            pl.BlockSpec((4, RET_CHUNK, LANES), lambda b, c: (2, row(b, c), 0)),
            _resident(decay.shape), _resident(xi.shape), _resident(zeta.shape),
            _resident(ret_norm.shape),
        ],
        out_specs=pl.BlockSpec((RET_CHUNK, RET_HEADS * RET_DV), lambda b, c: (row(b, c), 0)),
        out_shape=jax.ShapeDtypeStruct((t, RET_HEADS * RET_DV), BF16),
        scratch_shapes=[pltpu.VMEM((RET_HEADS, LANES, RET_DV), F32)],
        compiler_params=_params("parallel", "arbitrary"),
        name="retention",
    )(proj, proj, proj, proj, decay, xi, zeta, ret_norm)


def _diff_kernel(qi_ref, kj_ref, lq1_ref, lk1_ref, lq2_ref, lk2_ref, g_ref, q_ref, k_ref, vt_ref,
                 o_ref, s0_ref, s1_ref, m_ref, l_ref, acc_ref, *, n_stage):
    s_refs = (s0_ref, s1_ref)
    lane = lax.broadcasted_iota(jnp.int32, (ATT_TILE, LANES), 1)

    def rows(ref_idx):
        return pl.ds(pl.multiple_of(ref_idx * ATT_TILE, ATT_TILE), ATT_TILE)

    def reset():
        m_ref[...] = jnp.full_like(m_ref, -jnp.inf)
        l_ref[...] = jnp.zeros_like(l_ref)
        acc_ref[...] = jnp.zeros_like(acc_ref)

    def score(t, slot):
        q = q_ref[0, rows(qi_ref[t]), :]
        k = k_ref[0, rows(kj_ref[t]), :]
        zero = jnp.zeros_like(q)
        for c in range(2):
            own = (lane >= DIFF_DH) if c else (lane < DIFF_DH)
            s_refs[slot][c] = _dot_nt(k, jnp.where(own, q, zero))

    def softmax_pv(t, slot, diag):
        vt = vt_ref[:, rows(kj_ref[t])]
        for c in range(2):
            def scores():
                st = s_refs[slot][c]
                if diag:
                    key = lax.broadcasted_iota(jnp.int32, st.shape, 0)
                    qry = lax.broadcasted_iota(jnp.int32, st.shape, 1)
                    st = jnp.where(key <= qry, st, -jnp.inf)
                return st
            m_prev = m_ref[c]
            m_new = jnp.maximum(m_prev, jnp.max(scores(), axis=0, keepdims=True))
            alpha = jnp.exp2(m_prev - m_new)
            p = jnp.exp2(scores() - m_new)
            l_ref[c] = alpha * l_ref[c] + jnp.sum(p, axis=0, keepdims=True)
            acc_ref[c] = alpha * acc_ref[c] + _dot(vt, p.astype(BF16))
            m_ref[c] = m_new
        if diag:
            lam = (jnp.exp(jnp.sum(lq1_ref[...] * lk1_ref[...], keepdims=True))
                   - jnp.exp(jnp.sum(lq2_ref[...] * lk2_ref[...], keepdims=True)) + LAMBDA_INIT)
            out_t = acc_ref[0] / l_ref[0] - lam * (acc_ref[1] / l_ref[1])
            ms = jnp.mean(out_t * out_t, axis=0, keepdims=True)
            y = (out_t * lax.rsqrt(ms + EPS)).T * g_ref[0]
            o_ref[rows(qi_ref[t]), :] = (y * (1.0 - LAMBDA_INIT)).astype(BF16)
            reset()

    reset()
    score(0, 0)

    def body(t, carry):
        odd = (t % 2) == 1
        diag = qi_ref[t] == kj_ref[t]
        for slot in range(2):
            for is_diag in (False, True):
                slot_now = odd if slot else jnp.logical_not(odd)
                kind_now = diag if is_diag else jnp.logical_not(diag)

                @pl.when(slot_now & kind_now)
                def _():
                    score(t + 1, 1 - slot)
                    softmax_pv(t, slot, is_diag)
        return carry

    lax.fori_loop(0, n_stage - 1, body, 0)
    softmax_pv(n_stage - 1, (n_stage - 1) % 2, True)


def _diffattn(proj, vt, lq1, lk1, lq2, lk2, gain, batch, seq):
    t = batch * seq
    nq = seq // ATT_TILE
    pairs = [(i, j) for i in range(nq) for j in range(i + 1)]
    qi = jnp.asarray([p[0] for p in pairs], jnp.int32)
    kj = jnp.asarray([p[1] for p in pairs], jnp.int32)
    smem = pl.BlockSpec(memory_space=pltpu.SMEM)
    vec = pl.BlockSpec((1, DIFF_DH), lambda b, h: (0, 0))
    score_buf = pltpu.VMEM((2, ATT_TILE, ATT_TILE), F32)
    return pl.pallas_call(
        functools.partial(_diff_kernel, n_stage=len(pairs)),
        grid=(batch, DIFF_HEADS),
        in_specs=[
            smem, smem, vec, vec, vec, vec,
            pl.BlockSpec((1, 1, DIFF_DV), lambda b, h: (h, 0, 0)),
            pl.BlockSpec((1, seq, LANES), lambda b, h: (BLK_DQ + h, b, 0)),
            pl.BlockSpec((1, seq, LANES), lambda b, h: (BLK_DK + h, b, 0)),
            pl.BlockSpec((DIFF_DV, seq), lambda b, h: (h, b)),
        ],
        out_specs=pl.BlockSpec((seq, DIFF_DV), lambda b, h: (b, h)),
        out_shape=jax.ShapeDtypeStruct((t, DIFF_HEADS * DIFF_DV), BF16),
        scratch_shapes=[score_buf, score_buf,
                        pltpu.VMEM((2, 1, ATT_TILE), F32), pltpu.VMEM((2, 1, ATT_TILE), F32),
                        pltpu.VMEM((2, DIFF_DV, ATT_TILE), F32)],
        compiler_params=_params("parallel", "parallel"),
        name="diffattn",
    )(qi, kj, lq1, lk1, lq2, lk2, gain, proj, proj, vt)


def _kv_kernel(m_ref, g_ref, w_ref, o_ref):
    o_ref[...] = _dot(_rms(m_ref[...], g_ref[...]).astype(BF16), w_ref[...]).astype(BF16)


def _kvproj(mem, g, w):
    t, d = mem.shape
    return pl.pallas_call(
        _kv_kernel,
        grid=(t // TOKEN_TILE,),
        in_specs=[pl.BlockSpec((TOKEN_TILE, d), lambda i: (i, 0)), _resident(g.shape),
                  _resident(w.shape)],
        out_specs=pl.BlockSpec((TOKEN_TILE, w.shape[1]), lambda i: (i, 0)),
        out_shape=jax.ShapeDtypeStruct((t, w.shape[1]), BF16),
        compiler_params=_params("parallel"),
        name="kvproj",
    )(mem, g, w)


def _mid_kernel(x_ref, yr_ref, yd_ref, wout_ref, g_ref, wq_ref, kv_ref, wo_ref, o_ref):
    half = wout_ref.shape[0] // 2
    x = x_ref[...] + _dot(yr_ref[...], wout_ref[0:half, :]) + _dot(yd_ref[...], wout_ref[half:, :])
    h = _rms(x, g_ref[...]).astype(BF16)
    q = (_dot(h, wq_ref[...]) * (XATTN_DH ** -0.5)).astype(BF16)
    outs = []
    for hd in range(XATTN_HEADS):
        cols = slice(hd * XATTN_DH, (hd + 1) * XATTN_DH)
        k = kv_ref[:, cols]
        v = kv_ref[:, D_MODEL + hd * XATTN_DH:D_MODEL + (hd + 1) * XATTN_DH]
        s = _dot_nt(q[:, cols], k)
        p = jnp.exp(s - jnp.max(s, axis=-1, keepdims=True))
        o = _dot(p.astype(BF16), v) / jnp.sum(p, axis=-1, keepdims=True)
        outs.append(o.astype(BF16))
    o_ref[...] = x + _dot(jnp.concatenate(outs, axis=-1), wo_ref[...])


def _mid(x, yr, yd, wout, g, wq, kv, wo, seq):
    t, d = x.shape
    steps_per_seq = seq // TOKEN_TILE
    tile = lambda w: pl.BlockSpec((TOKEN_TILE, w), lambda i: (i, 0))
    return pl.pallas_call(
        _mid_kernel,
        grid=(t // TOKEN_TILE,),
        in_specs=[tile(d), tile(yr.shape[1]), tile(yd.shape[1]), _resident(wout.shape),
                  _resident(g.shape), _resident(wq.shape),
                  pl.BlockSpec((N_MEM, kv.shape[1]), lambda i: (i // steps_per_seq, 0)),
                  _resident(wo.shape)],
        out_specs=tile(d),
        out_shape=jax.ShapeDtypeStruct((t, d), F32),
        compiler_params=_params("parallel"),
        name="mid",
    )(x, yr, yd, wout, g, wq, kv, wo)


def kernel(x, mem, ffn1_norm, ffn1_w_gate, ffn1_w_up, ffn1_w_down, mix_norm, w_in, ret_out_norm, diff_lq1, diff_lk1, diff_lq2, diff_lk2, diff_out_norm, w_out, xattn_norm, mem_norm, xattn_wq, xattn_wkv, xattn_wo, ffn2_norm, ffn2_w_gate, ffn2_w_up, ffn2_w_down, final_norm):
    batch, seq, d = x.shape
    assert ffn1_norm.shape[0] == 1, "single-layer kernel"
    assert seq % ATT_TILE == 0 and seq % RET_CHUNK == 0 and seq % TOKEN_TILE == 0
    row = lambda v: v.reshape(1, -1).astype(F32)
    w16 = lambda w: w[0].astype(BF16)

    xt = x.reshape(batch * seq, d)
    final_g = row(final_norm)

    x1 = _ffn(xt, row(ffn1_norm[0]), w16(ffn1_w_gate), w16(ffn1_w_up), w16(ffn1_w_down),
              final_g, final=False)
    proj, vt = _inproj(x1, row(mix_norm[0]), w16(w_in), seq)
    y_ret = _retention(proj, ret_out_norm[0].astype(F32), batch, seq)
    y_diff = _diffattn(proj, vt, row(diff_lq1[0]), row(diff_lk1[0]), row(diff_lq2[0]), row(diff_lk2[0]),
                       diff_out_norm[0].reshape(DIFF_HEADS, 1, DIFF_DV).astype(F32), batch, seq)
    kv = _kvproj(mem.reshape(batch * N_MEM, d), row(mem_norm[0]), w16(xattn_wkv))
    x3 = _mid(x1, y_ret, y_diff, w16(w_out), row(xattn_norm[0]), w16(xattn_wq), kv,
              w16(xattn_wo), seq)
    y = _ffn(x3, row(ffn2_norm[0]), w16(ffn2_w_gate), w16(ffn2_w_up), w16(ffn2_w_down),
             final_g, final=True)
    return y.reshape(batch, seq, d)
```
